```python
import jax, jax.numpy as jnp
from jax import lax
import numpy as np

D_MODEL = 1024
BATCH = 8
SEQ = 8192
DEPTH = 1

CONV_DIM = 512
CONV_GROUPS = 8
CONV_K = 3
RWKV_HEADS = 8
RWKV_HEAD_DIM = 64
RWKV_DIM = RWKV_HEADS * RWKV_HEAD_DIM
DECAY_RANK = 64
ICLR_RANK = 64
GATE_RANK = 128
DECAY_SCALE = 0.6065306597126334
GN_EPS = 64e-5
CONV_PROJ = 3 * CONV_DIM
RWKV_PROJ = 3 * RWKV_DIM + DECAY_RANK + ICLR_RANK + GATE_RANK
GATE_PROJ = 2 * D_MODEL
IN_PROJ = CONV_PROJ + RWKV_PROJ + GATE_PROJ
PEER_HEADS = 8
PEER_NKEYS = 128
PEER_EXPERTS = PEER_NKEYS * PEER_NKEYS
PEER_QDIM = 256
PEER_HALF = PEER_QDIM // 2
PEER_TOPK = 16
PEER_TOKEN_BLOCK = 128
PLE_DIM = 256
NORM_EPS = 1e-6

kernel_name = "hybrid_conv_rwkv7_peer_block"


def rmsnorm(x, g):
    xf = x.astype(jnp.float32)
    y = xf * lax.rsqrt(jnp.mean(xf * xf, axis=-1, keepdims=True) + NORM_EPS)
    return y.astype(x.dtype) * g


def short_conv_mixer(z, conv_w, conv_b):
    gate_b, gate_c, x_in = jnp.split(z, 3, axis=-1)
    u = gate_c * x_in
    y = lax.conv_general_dilated(u, conv_w[:, None, :], window_strides=(1,), padding=[(CONV_K - 1, 0)],
                                 dimension_numbers=('NWC', 'WIO', 'NWC'), feature_group_count=CONV_DIM)
    return gate_b * (y + conv_b)


def rwkv7_scan(r, w, k, v, kk, a):
    bsz, _, nh, nd = r.shape
    xs = tuple(jnp.moveaxis(t, 1, 0) for t in (r, w, k, v, kk, a))

    def step(S, inp):
        r_t, w_t, k_t, v_t, kk_t, a_t = inp
        s_kk = jnp.einsum('bhvk,bhk->bhv', S, kk_t)
        S = (S * w_t[:, :, None, :] - s_kk[..., None] * (kk_t * a_t)[:, :, None, :]
             + v_t[..., None] * k_t[:, :, None, :])
        return S, jnp.einsum('bhvk,bhk->bhv', S, r_t)

    s0 = jnp.zeros((bsz, nh, nd, nd), jnp.float32)
    _, ys = lax.scan(step, s0, xs)
    return jnp.moveaxis(ys, 0, 1)


def rwkv7_mixer(z, shift_mu, w0, w_up, a0, a_up, g_up, k_k, k_a, r_k, ln_g, ln_b):
    bsz, seq, _ = z.shape
    f32 = jnp.float32
    z_prev = jnp.pad(z, ((0, 0), (1, 0), (0, 0)))[:, :-1]
    z = z + shift_mu * (z_prev - z)
    r, k, v, wd, ad, gd = jnp.split(
        z, [RWKV_DIM, 2 * RWKV_DIM, 3 * RWKV_DIM, 3 * RWKV_DIM + DECAY_RANK,
            3 * RWKV_DIM + DECAY_RANK + ICLR_RANK], axis=-1)
    d = (w0 + jnp.tanh(wd) @ w_up).astype(f32)
    decay = jnp.exp(-DECAY_SCALE * jax.nn.sigmoid(d))
    a = jax.nn.sigmoid(a0 + ad @ a_up)
    g = jax.nn.sigmoid(gd) @ g_up
    heads = lambda t: t.reshape(bsz, seq, RWKV_HEADS, RWKV_HEAD_DIM).astype(f32)
    kk = heads(k * k_k)
    kk = kk * lax.rsqrt(jnp.sum(kk * kk, axis=-1, keepdims=True) + 1e-12)
    k = k * (1.0 + (a - 1.0) * k_a)
    rh, kh, vh, ah, wh = heads(r), heads(k), heads(v), heads(a), heads(decay)
    y = rwkv7_scan(rh, wh, kh, vh, kk, ah)
    mu = jnp.mean(y, axis=-1, keepdims=True)
    var = jnp.mean(jnp.square(y - mu), axis=-1, keepdims=True)
    y = ((y - mu) * lax.rsqrt(var + GN_EPS)).reshape(bsz, seq, RWKV_DIM) * ln_g + ln_b
    bonus = jnp.sum(rh * kh * r_k.astype(f32), axis=-1, keepdims=True) * vh
    y = (y + bonus.reshape(bsz, seq, RWKV_DIM)) * g
    return y.astype(z.dtype)


def peer_ffn(u, wq, subkeys, tab_u, tab_v):
    bsz, seq, dm = u.shape
    blocks = u.reshape(-1, PEER_TOKEN_BLOCK, dm)

    def one_block(xb):
        tb = xb.shape[0]
        q = (xb @ wq).reshape(tb, PEER_HEADS, 2, PEER_HALF)
        s = jnp.einsum('thcd,hcnd->thcn', q, subkeys)
        sv, si = lax.top_k(s, PEER_TOPK)
        cand_s = (sv[:, :, 0, :, None] + sv[:, :, 1, None, :]).reshape(tb, PEER_HEADS, PEER_TOPK * PEER_TOPK)
        cand_i = (si[:, :, 0, :, None] * PEER_NKEYS + si[:, :, 1, None, :]).reshape(tb, PEER_HEADS, PEER_TOPK * PEER_TOPK)
        top_s, pos = lax.top_k(cand_s, PEER_TOPK)
        idx = jnp.take_along_axis(cand_i, pos, axis=-1)
        gate = jax.nn.softmax(top_s.astype(jnp.float32), axis=-1).astype(xb.dtype)
        hid = jax.nn.gelu(jnp.einsum('td,thkd->thk', xb, tab_u[idx]))
        return jnp.einsum('thk,thkd->td', gate * hid, tab_v[idx])

    return lax.map(one_block, blocks).reshape(bsz, seq, dm)


def setup_inputs(seed: int = 0) -> dict:
    key = jax.random.key(seed)
    ks = iter(jax.random.split(key, 40))
    nrm = lambda shape, scale: jax.random.normal(next(ks), shape, jnp.float32) * scale
    L = DEPTH
    return {
        'x': nrm((BATCH, SEQ, D_MODEL), 1.0),
        'p': nrm((DEPTH, BATCH, SEQ, PLE_DIM), 1.0),
        'norm_mix_g': 1.0 + nrm((L, D_MODEL), 0.02),
        'w_in': nrm((L, D_MODEL, IN_PROJ), D_MODEL ** -0.5),
        'conv_w': nrm((L, CONV_K, CONV_DIM), CONV_K ** -0.5),
        'conv_b': nrm((L, CONV_DIM), 0.01),
        'shift_mu': jax.random.uniform(next(ks), (L, RWKV_PROJ), jnp.float32),
        'w0': nrm((L, RWKV_DIM), 0.5),
        'w_up': nrm((L, DECAY_RANK, RWKV_DIM), 0.5 * DECAY_RANK ** -0.5),
        'a0': nrm((L, RWKV_DIM), 0.5),
        'a_up': nrm((L, ICLR_RANK, RWKV_DIM), 0.5 * ICLR_RANK ** -0.5),
        'g_up': nrm((L, GATE_RANK, RWKV_DIM), GATE_RANK ** -0.5),
        'k_k': 0.85 + nrm((L, RWKV_DIM), 0.05),
        'k_a': 1.0 + nrm((L, RWKV_DIM), 0.05),
        'r_k': nrm((L, RWKV_HEADS, RWKV_HEAD_DIM), 0.1),
        'ln_x_g': 1.0 + nrm((L, RWKV_DIM), 0.02),
        'ln_x_b': nrm((L, RWKV_DIM), 0.01),
        'w_branch_a': nrm((L, CONV_DIM, D_MODEL), CONV_DIM ** -0.5),
        'w_branch_b': nrm((L, RWKV_DIM, D_MODEL), RWKV_DIM ** -0.5),
        'w_out': nrm((L, D_MODEL, D_MODEL), D_MODEL ** -0.5),
        'norm_ffn_g': 1.0 + nrm((L, D_MODEL), 0.02),
        'peer_wq': nrm((L, D_MODEL, PEER_HEADS * PEER_QDIM), D_MODEL ** -0.5),
        'peer_subkeys': nrm((L, PEER_HEADS, 2, PEER_NKEYS, PEER_HALF), PEER_HALF ** -0.5),
        'peer_u': nrm((L, PEER_EXPERTS, D_MODEL), D_MODEL ** -0.5),
        'peer_v': nrm((L, PEER_EXPERTS, D_MODEL), (PEER_HEADS * PEER_TOPK) ** -0.5),
        'norm_ple_g': 1.0 + nrm((L, D_MODEL), 0.02),
        'ple_gate_w': nrm((L, D_MODEL, D_MODEL), D_MODEL ** -0.5),
        'ple_proj_w': nrm((L, PLE_DIM, D_MODEL), PLE_DIM ** -0.5),
        'final_norm_g': 1.0 + nrm((D_MODEL,), 0.02),
    }


def reference(x, p, norm_mix_g, w_in, conv_w, conv_b, shift_mu, w0, w_up, a0, a_up, g_up,
              k_k, k_a, r_k, ln_x_g, ln_x_b, w_branch_a, w_branch_b, w_out, norm_ffn_g,
              peer_wq, peer_subkeys, peer_u, peer_v, norm_ple_g, ple_gate_w, ple_proj_w,
              final_norm_g):
    h = x
    for i in range(DEPTH):
        xn = rmsnorm(h, norm_mix_g[i])
        z = xn @ w_in[i]
        z_conv = z[..., :CONV_PROJ]
        z_rwkv = z[..., CONV_PROJ:CONV_PROJ + RWKV_PROJ]
        gate_a, gate_b = jnp.split(z[..., CONV_PROJ + RWKV_PROJ:], 2, axis=-1)
        y_a = short_conv_mixer(z_conv, conv_w[i], conv_b[i]) @ w_branch_a[i]
        y_b = rwkv7_mixer(z_rwkv, shift_mu[i], w0[i], w_up[i], a0[i], a_up[i], g_up[i],
                          k_k[i], k_a[i], r_k[i], ln_x_g[i], ln_x_b[i]) @ w_branch_b[i]
        merged = jax.nn.sigmoid(gate_a) * y_a + jax.nn.sigmoid(gate_b) * y_b
        h = h + merged @ w_out[i]
        h = h + peer_ffn(rmsnorm(h, norm_ffn_g[i]), peer_wq[i], peer_subkeys[i], peer_u[i], peer_v[i])
        ple_gate = jax.nn.sigmoid(rmsnorm(h, norm_ple_g[i]) @ ple_gate_w[i])
        h = h + ple_gate * (p[i] @ ple_proj_w[i])
    return rmsnorm(h, final_norm_g)
```

```python
import functools

import jax
import jax.numpy as jnp
import numpy as np
from jax import lax
from jax.experimental import pallas as pl
from jax.experimental.pallas import tpu as pltpu

D_MODEL = 1024
CONV_DIM = 512
RWKV_HEADS = 8
HEAD_DIM = 64
RWKV_DIM = RWKV_HEADS * HEAD_DIM
DECAY_RANK = 64
ICLR_RANK = 64
GATE_RANK = 128
DECAY_SCALE = 0.6065306597126334
GN_EPS = 64e-5
NORM_EPS = 1e-6
CONV_PROJ = 3 * CONV_DIM
RWKV_PROJ = 3 * RWKV_DIM + DECAY_RANK + ICLR_RANK + GATE_RANK
PEER_HEADS = 8
PEER_NKEYS = 128
PEER_HALF = 128
PEER_TOPK = 16
PLE_DIM = 256

LANES = 128
SUBLANES = 8
VMEM_LIMIT = 56 * 1024 * 1024

HI = lax.Precision.HIGHEST
F32 = jnp.float32
BF16 = jnp.bfloat16


def _dot(a, b, precision=None):
    return jnp.dot(a, b, preferred_element_type=F32, precision=precision)


def _dot_nt(a, b, precision=None):
    return lax.dot_general(a, b, (((1,), (1,)), ((), ())), preferred_element_type=F32, precision=precision)


def _dot_tn(a, b, precision=None):
    return lax.dot_general(a, b, (((0,), (0,)), ((), ())), preferred_element_type=F32, precision=precision)


def _rms(x, g):
    return x * lax.rsqrt(jnp.mean(x * x, axis=-1, keepdims=True) + NORM_EPS) * g


def _shift_rows(x, prev_rows, n):
    rolled = pltpu.roll(x, n, axis=0)
    head = pltpu.roll(prev_rows, n, axis=0)
    row = lax.broadcasted_iota(jnp.int32, x.shape, 0)
    head_full = jnp.concatenate([head, rolled[SUBLANES:]], axis=0)
    return jnp.where(row < n, head_full, rolled)


def _head_block_ones():
    r = lax.broadcasted_iota(jnp.int32, (RWKV_DIM, RWKV_DIM), 0) // HEAD_DIM
    c = lax.broadcasted_iota(jnp.int32, (RWKV_DIM, RWKV_DIM), 1) // HEAD_DIM
    return (r == c).astype(F32)


def _mix_in_kernel(x_ref, ng_ref, win_ref, cw_ref, cb_ref, mu_ref, w0_ref, wup_ref, a0_ref, aup_ref, gup_ref,
                   kk_ref, ka_ref, rk_ref, wa_ref,
                   ma_ref, sgb_ref, r_ref, lw_ref, k_ref, v_ref, kkn_ref, a_ref, g_ref, bonus_ref,
                   u_carry, z_carry):
    t_blk = pl.program_id(1)

    @pl.when(t_blk == 0)
    def _():
        u_carry[...] = jnp.zeros_like(u_carry)
        z_carry[...] = jnp.zeros_like(z_carry)

    x = x_ref[0]
    xn = _rms(x, ng_ref[...]).astype(BF16)
    z_conv = _dot(xn, win_ref[:, :CONV_PROJ])
    z_rwkv = _dot(xn, win_ref[:, CONV_PROJ:CONV_PROJ + RWKV_PROJ])
    gates = _dot(xn, win_ref[:, CONV_PROJ + RWKV_PROJ:])

    gate_b = z_conv[:, :CONV_DIM]
    u = z_conv[:, CONV_DIM:2 * CONV_DIM] * z_conv[:, 2 * CONV_DIM:]
    prev_u = u_carry[...]
    u1 = _shift_rows(u, prev_u, 1)
    u2 = _shift_rows(u, prev_u, 2)
    u_carry[...] = u[-SUBLANES:]
    cw = cw_ref[...]
    conv = cw[0:1] * u2 + cw[1:2] * u1 + cw[2:3] * u
    ya = (gate_b * (conv + cb_ref[...])).astype(BF16)
    y_a = _dot(ya, wa_ref[...])
    ma_ref[0] = jax.nn.sigmoid(gates[:, :D_MODEL]) * y_a
    sgb_ref[0] = jax.nn.sigmoid(gates[:, D_MODEL:])

    z_prev = _shift_rows(z_rwkv, z_carry[...], 1)
    z_carry[...] = z_rwkv[-SUBLANES:]
    zs = z_rwkv + mu_ref[...] * (z_prev - z_rwkv)
    r = zs[:, :RWKV_DIM]
    k = zs[:, RWKV_DIM:2 * RWKV_DIM]
    v = zs[:, 2 * RWKV_DIM:3 * RWKV_DIM]
    wa_d = zs[:, 3 * RWKV_DIM:3 * RWKV_DIM + DECAY_RANK + ICLR_RANK]
    gd = zs[:, 3 * RWKV_DIM + DECAY_RANK + ICLR_RANK:]
    d = w0_ref[...] + _dot(jnp.tanh(wa_d), wup_ref[...], HI)
    lw_ref[0] = -DECAY_SCALE * jax.nn.sigmoid(d)
    a = jax.nn.sigmoid(a0_ref[...] + _dot(wa_d, aup_ref[...], HI))
    g_ref[0] = _dot(jax.nn.sigmoid(gd), gup_ref[...], HI)
    ones = _head_block_ones()
    kk = k * kk_ref[...]
    kkn_ref[0] = kk * lax.rsqrt(_dot(kk * kk, ones, HI) + 1e-12)
    k2 = k * (1.0 + (a - 1.0) * ka_ref[...])
    bonus_ref[0] = _dot(r * k2 * rk_ref[...], ones, HI) * v
    r_ref[0] = r
    k_ref[0] = k2
    v_ref[0] = v
    a_ref[0] = a


def _mix_in(x, ng, win, cw, cb, mu, w0, wup, a0, aup, gup, kk, ka, rk, wa, tb):
    bsz, seq, _ = x.shape
    full = lambda arr: pl.BlockSpec(arr.shape, lambda b, t: (0,) * arr.ndim)
    tok = lambda c: pl.BlockSpec((1, tb, c), lambda b, t: (b, t, 0))
    params = (ng, win, cw, cb, mu, w0, wup, a0, aup, gup, kk, ka, rk, wa)
    out_dims = (D_MODEL, D_MODEL) + (RWKV_DIM,) * 8
    return pl.pallas_call(
        _mix_in_kernel,
        grid=(bsz, seq // tb),
        in_specs=[tok(D_MODEL)] + [full(p) for p in params],
        out_specs=[tok(c) for c in out_dims],
        out_shape=[jax.ShapeDtypeStruct((bsz, seq, c), F32) for c in out_dims],
        scratch_shapes=[pltpu.VMEM((SUBLANES, CONV_DIM), F32), pltpu.VMEM((SUBLANES, RWKV_PROJ), F32)],
        compiler_params=pltpu.CompilerParams(dimension_semantics=("arbitrary", "arbitrary"),
                                             vmem_limit_bytes=VMEM_LIMIT),
        name="mix_in",
    )(x, *params)


RW_CHUNK = 64
RW_GROUP = 4
RW_LANES = RW_GROUP * HEAD_DIM
RW_ROWS = RW_GROUP * RW_CHUNK
assert RW_CHUNK == HEAD_DIM


def _rwkv_kernel(r_ref, lw_ref, k_ref, v_ref, kk_ref, a_ref, y_ref, s_ref):
    n, lc = RW_ROWS, RW_CHUNK

    @pl.when(pl.program_id(2) == 0)
    def _():
        s_ref[...] = jnp.zeros_like(s_ref)

    r, lw, k, v, kk, a = (ref[0] for ref in (r_ref, lw_ref, k_ref, v_ref, kk_ref, a_ref))
    row = lax.broadcasted_iota(jnp.int32, (n, n), 0)
    col = lax.broadcasted_iota(jnp.int32, (n, n), 1)
    same = (row // lc) == (col // lc)
    strict = same & (col < row)
    incl = same & (col <= row)
    tri = (lax.broadcasted_iota(jnp.int32, (lc, lc), 1) <= lax.broadcasted_iota(jnp.int32, (lc, lc), 0)).astype(F32)

    cum = _dot(tri, lw, HI)
    last = cum[lc - 1:lc]
    gam = jnp.exp(cum)
    ginv = jnp.exp(-cum)
    gto_end = jnp.exp(last - cum)
    tile = lambda t: jnp.concatenate([t] * RW_GROUP, axis=0)
    stack = lambda t: jnp.where(same, tile(t), 0.0)
    q = a * kk
    pt_s = stack(-kk * jnp.exp(cum - lw))
    rt_s = stack(r * gam)
    qh_t = tile(q * ginv)
    kh_t = tile(k * ginv)
    v_s = stack(v)

    mpq = jnp.where(strict, _dot_nt(pt_s, qh_t, HI), 0.0)
    mpk = jnp.where(strict, _dot_nt(pt_s, kh_t, HI), 0.0)
    arq = jnp.where(incl, _dot_nt(rt_s, qh_t, HI), 0.0)
    ark = jnp.where(incl, _dot_nt(rt_s, kh_t, HI), 0.0)

    tm = jnp.where(row == col, 1.0, 0.0) + mpq
    mp = mpq
    for _ in range(int(np.log2(lc)) - 1):
        mp = _dot(mp, mp, HI)
        tm = tm + _dot(tm, mp, HI)

    s = s_ref[...]
    u = _dot(tm, _dot_nt(pt_s, s, HI) + _dot(mpk, v_s, HI), HI)
    y_s = _dot_nt(rt_s, s, HI) + _dot(arq, u, HI) + _dot(ark, v_s, HI)
    y = y_s[0:lc]
    for g in range(1, RW_GROUP):
        y = y + y_s[g * lc:(g + 1) * lc]
    y_ref[0] = y
    s_ref[...] = jnp.exp(last) * s + _dot_tn(u, stack(q * gto_end), HI) + _dot_tn(v_s, stack(k * gto_end), HI)


def _rwkv_scan(r, lw, k, v, kk, a):
    bsz, seq, _ = r.shape
    spec = pl.BlockSpec((1, RW_CHUNK, RW_LANES), lambda b, g, c: (b, c, g))
    return pl.pallas_call(
        _rwkv_kernel,
        grid=(bsz, RWKV_DIM // RW_LANES, seq // RW_CHUNK),
        in_specs=[spec] * 6,
        out_specs=spec,
        out_shape=jax.ShapeDtypeStruct((bsz, seq, RWKV_DIM), F32),
        scratch_shapes=[pltpu.VMEM((RW_ROWS, RW_LANES), F32)],
        compiler_params=pltpu.CompilerParams(dimension_semantics=("arbitrary", "arbitrary", "arbitrary"),
                                             vmem_limit_bytes=VMEM_LIMIT),
        name="rwkv_scan",
    )(r, lw, k, v, kk, a)


def _mix_out_kernel(x_ref, y_ref, bonus_ref, g_ref, ma_ref, sgb_ref, lng_ref, lnb_ref, wb_ref, wo_ref, h_ref):
    y = y_ref[...]
    mean_mat = _head_block_ones() * (1.0 / HEAD_DIM)
    d = y - _dot(y, mean_mat, HI)
    var = _dot(d * d, mean_mat, HI)
    yn = d * lax.rsqrt(var + GN_EPS) * lng_ref[...] + lnb_ref[...]
    yb = ((yn + bonus_ref[...]) * g_ref[...]).astype(BF16)
    merged = ma_ref[...] + sgb_ref[...] * _dot(yb, wb_ref[...])
    h_ref[...] = x_ref[...] + _dot(merged.astype(BF16), wo_ref[...])


def _mix_out(x, y, bonus, g, ma, sgb, lng, lnb, wb, wo, tb):
    ntok = x.shape[0]
    full = lambda arr: pl.BlockSpec(arr.shape, lambda t: (0,) * arr.ndim)
    tok = lambda c: pl.BlockSpec((tb, c), lambda t: (t, 0))
    return pl.pallas_call(
        _mix_out_kernel,
        grid=(ntok // tb,),
        in_specs=[tok(D_MODEL)] + [tok(RWKV_DIM)] * 3 + [tok(D_MODEL)] * 2 + [full(p) for p in (lng, lnb, wb, wo)],
        out_specs=tok(D_MODEL),
        out_shape=jax.ShapeDtypeStruct((ntok, D_MODEL), F32),
        compiler_params=pltpu.CompilerParams(dimension_semantics=("arbitrary",), vmem_limit_bytes=VMEM_LIMIT),
        name="mix_out",
    )(x, y, bonus, g, ma, sgb, lng, lnb, wb, wo)


def _sort16_pairs():
    n, pairs, p = PEER_TOPK, [], 1
    while p < n:
        k = p
        while k >= 1:
            for j in range(k % p, n - k, 2 * k):
                for i in range(min(k, n - j - k)):
                    if (i + j) // (2 * p) == (i + j + k) // (2 * p):
                        pairs.append((i + j, i + j + k))
            k //= 2
        p *= 2
    return pairs


def _vmax(a, b):
    return b if a is None else a if b is None else jnp.maximum(a, b)


def _vmin(a, b):
    return None if a is None or b is None else jnp.minimum(a, b)


def _merge_top16(a, b):
    pad = lambda t: list(t) + [None] * (PEER_TOPK - len(t))
    a, b = pad(a), pad(b)
    c = [_vmax(a[i], b[PEER_TOPK - 1 - i]) for i in range(PEER_TOPK)]
    d = PEER_TOPK // 2
    while d >= 1:
        for i in range(PEER_TOPK):
            if i & d == 0:
                c[i], c[i + d] = _vmax(c[i], c[i + d]), _vmin(c[i], c[i + d])
        d //= 2
    return [t for t in c if t is not None]


def _top16_sorted(vals):
    groups = []
    for g0 in range(0, len(vals), PEER_TOPK):
        grp = list(vals[g0:g0 + PEER_TOPK])
        for i, j in _sort16_pairs():
            grp[i], grp[j] = jnp.maximum(grp[i], grp[j]), jnp.minimum(grp[i], grp[j])
        groups.append(grp)
    while len(groups) > 1:
        groups = [_merge_top16(groups[i], groups[i + 1]) for i in range(0, len(groups), 2)]
    return groups[0]


def _pair_top16_sorted(a, b):
    k = PEER_TOPK
    lists = [[a[r] + b[j] for r in range(k // (j + 1))] for j in range(k // 2)]
    lists.append([a[0] + b[j] for j in range(k // 2, k)])
    top = lists[0]
    for t in lists[1:]:
        top = _merge_top16(top, t)
    return top


def _peer_topk_kernel(h_ref, g_ref, wq_ref, sk_ref, s_ref, tau_ref, sh0_ref, sh1_ref, stage_ref):
    ngrp = h_ref.shape[0] // LANES
    xn = _rms(h_ref[...], g_ref[...]).astype(BF16)
    for h in range(PEER_HEADS):
        qh = _dot(xn, wq_ref[:, h * 2 * PEER_HALF:(h + 1) * 2 * PEER_HALF])
        halves = []
        for c in range(2):
            st = _dot_nt(sk_ref[2 * h + c], qh[:, c * PEER_HALF:(c + 1) * PEER_HALF], HI)
            for g in range(ngrp):
                blk = st[:, g * LANES:(g + 1) * LANES]
                s_ref[2 * h + c, g] = blk
                stage_ref[pl.ds(g, PEER_NKEYS, stride=ngrp), :] = blk
            halves.append(_top16_sorted([stage_ref[pl.ds(key * ngrp, ngrp), :] for key in range(PEER_NKEYS)]))
        a, b = halves
        top = _pair_top16_sorted(a, b)
        z = jnp.ones_like(top[0])
        for t in top[1:]:
            z = z + jnp.exp(t - top[0])
        sh0 = a[0] + jnp.log(z)
        for g in range(ngrp):
            tau_ref[g, h:h + 1, :] = top[PEER_TOPK - 1][g:g + 1]
            sh0_ref[g, h:h + 1, :] = sh0[g:g + 1]
            sh1_ref[g, h:h + 1, :] = b[0][g:g + 1]


def _peer_topk(h1, g, wq, sk, tb):
    ntok = h1.shape[0]
    ngrp = tb // LANES
    full = lambda arr: pl.BlockSpec(arr.shape, lambda t: (0,) * arr.ndim)
    row_spec = pl.BlockSpec((ngrp, PEER_HEADS, LANES), lambda t: (t, 0, 0))
    row_shape = jax.ShapeDtypeStruct((ntok // LANES, PEER_HEADS, LANES), F32)
    return pl.pallas_call(
        _peer_topk_kernel,
        grid=(ntok // tb,),
        in_specs=[pl.BlockSpec((tb, D_MODEL), lambda t: (t, 0)), full(g), full(wq), full(sk)],
        out_specs=[pl.BlockSpec((2 * PEER_HEADS, ngrp, PEER_NKEYS, LANES), lambda t: (0, t, 0, 0)),
                   row_spec, row_spec, row_spec],
        out_shape=[jax.ShapeDtypeStruct((2 * PEER_HEADS, ntok // LANES, PEER_NKEYS, LANES), F32),
                   row_shape, row_shape, row_shape],
        scratch_shapes=[pltpu.VMEM((PEER_NKEYS * ngrp, LANES), F32)],
        compiler_params=pltpu.CompilerParams(dimension_semantics=("arbitrary",), vmem_limit_bytes=VMEM_LIMIT),
        name="peer_topk",
    )(h1, g, wq, sk)


def _gelu_tanh(x):
    return x * (0.5 * (1.0 + jnp.tanh(np.sqrt(2.0 / np.pi) * (x + 0.044715 * (x * x * x)))))


def _peer_ffn_kernel(h_ref, g_ref, s_ref, tau_ref, sh0_ref, sh1_ref, u_ref, vt_ref, o_ref,
                     xn_ref, acc_ref, e_ref, ht_ref, p_ref):
    e = pl.program_id(1)
    ngrp = h_ref.shape[0] // LANES
    nkey = u_ref.shape[0] // PEER_NKEYS

    @pl.when(e == 0)
    def _():
        xn_ref[...] = _rms(h_ref[...], g_ref[...]).astype(BF16)
        acc_ref[...] = jnp.zeros_like(acc_ref)
        for hc in range(2 * PEER_HEADS):
            sh_ref = sh0_ref if hc % 2 == 0 else sh1_ref
            for g in range(ngrp):
                e_ref[hc, g] = jnp.exp(s_ref[hc, g] - sh_ref[g, hc // 2:hc // 2 + 1, :])

    ht = _dot_nt(u_ref[...], xn_ref[...])
    for g in range(ngrp):
        ht_ref[g] = ht[:, g * LANES:(g + 1) * LANES]

    def group_body(g, carry):
        for il in range(nkey):
            i = e * nkey + il
            w = jnp.zeros((PEER_NKEYS, LANES), F32)
            for h in range(PEER_HEADS):
                pair = s_ref[2 * h + 1, g] + s_ref[2 * h, g, pl.ds(i, 1), :]
                gate = e_ref[2 * h + 1, g] * e_ref[2 * h, g, pl.ds(i, 1), :]
                w = w + jnp.where(pair >= tau_ref[g, h:h + 1, :], gate, 0.0)
            rows = slice(il * PEER_NKEYS, (il + 1) * PEER_NKEYS)
            p_ref[g, rows, :] = (_gelu_tanh(ht_ref[g, rows, :]) * w).astype(BF16)
        return carry

    lax.fori_loop(0, ngrp, group_body, 0)
    p = jnp.concatenate([p_ref[g] for g in range(ngrp)], axis=1)
    acc_ref[...] += _dot(vt_ref[...], p)

    @pl.when(e == pl.num_programs(1) - 1)
    def _():
        o_ref[...] = h_ref[...] + acc_ref[...].T


def _peer_ffn(h1, g, s, tau, sh0, sh1, u, vt, tb, ec):
    ntok = h1.shape[0]
    ngrp = tb // LANES
    row_spec = pl.BlockSpec((ngrp, PEER_HEADS, LANES), lambda t, e: (t, 0, 0))
    return pl.pallas_call(
        _peer_ffn_kernel,
        grid=(ntok // tb, u.shape[0] // ec),
        in_specs=[pl.BlockSpec((tb, D_MODEL), lambda t, e: (t, 0)),
                  pl.BlockSpec(g.shape, lambda t, e: (0, 0)),
                  pl.BlockSpec((2 * PEER_HEADS, ngrp, PEER_NKEYS, LANES), lambda t, e: (0, t, 0, 0)),
                  row_spec, row_spec, row_spec,
                  pl.BlockSpec((ec, D_MODEL), lambda t, e: (e, 0)),
                  pl.BlockSpec((D_MODEL, ec), lambda t, e: (0, e))],
        out_specs=pl.BlockSpec((tb, D_MODEL), lambda t, e: (t, 0)),
        out_shape=jax.ShapeDtypeStruct((ntok, D_MODEL), F32),
        scratch_shapes=[pltpu.VMEM((tb, D_MODEL), BF16),
                        pltpu.VMEM((D_MODEL, tb), F32),
                        pltpu.VMEM((2 * PEER_HEADS, ngrp, PEER_NKEYS, LANES), F32),
                        pltpu.VMEM((ngrp, ec, LANES), F32),
                        pltpu.VMEM((ngrp, ec, LANES), BF16)],
        compiler_params=pltpu.CompilerParams(dimension_semantics=("arbitrary", "arbitrary"),
                                             vmem_limit_bytes=VMEM_LIMIT),
        name="peer_ffn",
    )(h1, g, s, tau, sh0, sh1, u, vt)


def _ple_out_kernel(h_ref, p_ref, g_ref, wg_ref, wp_ref, fg_ref, o_ref, *, final):
    h = h_ref[...]
    gate = jax.nn.sigmoid(_dot(_rms(h, g_ref[...]).astype(BF16), wg_ref[...]))
    h = h + gate * _dot(p_ref[...].astype(BF16), wp_ref[...])
    o_ref[...] = _rms(h, fg_ref[...]) if final else h


def _ple_out(h, p, g, wg, wp, fg, tb, final):
    ntok = h.shape[0]
    full = lambda arr: pl.BlockSpec(arr.shape, lambda t: (0,) * arr.ndim)
    return pl.pallas_call(
        functools.partial(_ple_out_kernel, final=final),
        grid=(ntok // tb,),
        in_specs=[pl.BlockSpec((tb, D_MODEL), lambda t: (t, 0)), pl.BlockSpec((tb, PLE_DIM), lambda t: (t, 0)),
                  full(g), full(wg), full(wp), full(fg)],
        out_specs=pl.BlockSpec((tb, D_MODEL), lambda t: (t, 0)),
        out_shape=jax.ShapeDtypeStruct((ntok, D_MODEL), F32),
        compiler_params=pltpu.CompilerParams(dimension_semantics=("arbitrary",), vmem_limit_bytes=VMEM_LIMIT),
        name="ple_out",
    )(h, p, g, wg, wp, fg)


MIX_IN_TOKENS = 256
MIX_OUT_TOKENS = 512
TOPK_TOKENS = 1024
FFN_TOKENS = 512
FFN_EXPERTS = 1024
PLE_TOKENS = 512


def kernel(x, p, norm_mix_g, w_in, conv_w, conv_b, shift_mu, w0, w_up, a0, a_up, g_up, k_k, k_a, r_k, ln_x_g, ln_x_b,
           w_branch_a, w_branch_b, w_out, norm_ffn_g, peer_wq, peer_subkeys, peer_u, peer_v, norm_ple_g, ple_gate_w,
           ple_proj_w, final_norm_g):
    bsz, seq, _ = x.shape
    ntok = bsz * seq
    row = lambda t: t.reshape(1, -1)
    h = x
    for i in range(w_in.shape[0]):
        zeros_rank = jnp.zeros((DECAY_RANK, RWKV_DIM), F32)
        ma, sgb, r, lw, k, v, kk, a, g, bonus = _mix_in(
            h, row(norm_mix_g[i]), w_in[i].astype(BF16), conv_w[i], row(conv_b[i]), row(shift_mu[i]), row(w0[i]),
            jnp.concatenate([w_up[i], zeros_rank], axis=0), row(a0[i]),
            jnp.concatenate([zeros_rank, a_up[i]], axis=0), g_up[i], row(k_k[i]), row(k_a[i]), row(r_k[i]),
            w_branch_a[i].astype(BF16), MIX_IN_TOKENS)
        y = _rwkv_scan(r, lw, k, v, kk, a)
        flat = lambda t: t.reshape(ntok, t.shape[-1])
        h1 = _mix_out(flat(h), flat(y), flat(bonus), flat(g), flat(ma), flat(sgb), row(ln_x_g[i]), row(ln_x_b[i]),
                      w_branch_b[i].astype(BF16), w_out[i].astype(BF16), MIX_OUT_TOKENS)
        s, tau, sh0, sh1 = _peer_topk(h1, row(norm_ffn_g[i]), peer_wq[i].astype(BF16),
                                      peer_subkeys[i].reshape(2 * PEER_HEADS, PEER_NKEYS, PEER_HALF), TOPK_TOKENS)
        h2 = _peer_ffn(h1, row(norm_ffn_g[i]), s, tau, sh0, sh1, peer_u[i].astype(BF16),
                       peer_v[i].T.astype(BF16), FFN_TOKENS, FFN_EXPERTS)
        h = _ple_out(h2, flat(p[i]), row(norm_ple_g[i]), ple_gate_w[i].astype(BF16), ple_proj_w[i].astype(BF16),
                     row(final_norm_g), PLE_TOKENS, final=(i == w_in.shape[0] - 1))
        h = h.reshape(bsz, seq, D_MODEL)
    return h
```

```python
import functools

import jax
import jax.numpy as jnp
import numpy as np
from jax import lax
from jax.experimental import pallas as pl
from jax.experimental.pallas import tpu as pltpu

D_MODEL = 1024
CONV_DIM = 512
RWKV_HEADS = 8
HEAD_DIM = 64
RWKV_DIM = RWKV_HEADS * HEAD_DIM
DECAY_RANK = 64
ICLR_RANK = 64
GATE_RANK = 128
DECAY_SCALE = 0.6065306597126334
GN_EPS = 64e-5
NORM_EPS = 1e-6
CONV_PROJ = 3 * CONV_DIM
RWKV_PROJ = 3 * RWKV_DIM + DECAY_RANK + ICLR_RANK + GATE_RANK
PEER_HEADS = 8
PEER_NKEYS = 128
PEER_HALF = 128
PEER_TOPK = 16
PLE_DIM = 256

LANES = 128
SUBLANES = 8
VMEM_LIMIT = 56 * 1024 * 1024

HI = lax.Precision.HIGHEST
F32 = jnp.float32
BF16 = jnp.bfloat16


def _dot(a, b, precision=None):
    return jnp.dot(a, b, preferred_element_type=F32, precision=precision)


def _dot_nt(a, b, precision=None):
    return lax.dot_general(a, b, (((1,), (1,)), ((), ())), preferred_element_type=F32, precision=precision)


def _dot_tn(a, b, precision=None):
    return lax.dot_general(a, b, (((0,), (0,)), ((), ())), preferred_element_type=F32, precision=precision)


def _rms(x, g):
    return x * lax.rsqrt(jnp.mean(x * x, axis=-1, keepdims=True) + NORM_EPS) * g


def _shift_rows(x, prev_rows, n):
    rolled = pltpu.roll(x, n, axis=0)
    head = pltpu.roll(prev_rows, n, axis=0)
    row = lax.broadcasted_iota(jnp.int32, x.shape, 0)
    head_full = jnp.concatenate([head, rolled[SUBLANES:]], axis=0)
    return jnp.where(row < n, head_full, rolled)


def _head_block_ones():
    r = lax.broadcasted_iota(jnp.int32, (RWKV_DIM, RWKV_DIM), 0) // HEAD_DIM
    c = lax.broadcasted_iota(jnp.int32, (RWKV_DIM, RWKV_DIM), 1) // HEAD_DIM
    return (r == c).astype(F32)


def _mix_in_kernel(x_ref, ng_ref, win_ref, cw_ref, cb_ref, mu_ref, w0_ref, wup_ref, a0_ref, aup_ref, gup_ref,
                   kk_ref, ka_ref, rk_ref, wa_ref,
                   ma_ref, sgb_ref, r_ref, lw_ref, k_ref, v_ref, kkn_ref, a_ref, g_ref, bonus_ref,
                   u_carry, z_carry):
    t_blk = pl.program_id(1)

    @pl.when(t_blk == 0)
    def _():
        u_carry[...] = jnp.zeros_like(u_carry)
        z_carry[...] = jnp.zeros_like(z_carry)

    x = x_ref[0]
    xn = _rms(x, ng_ref[...]).astype(BF16)
    z_conv = _dot(xn, win_ref[:, :CONV_PROJ])
    z_rwkv = _dot(xn, win_ref[:, CONV_PROJ:CONV_PROJ + RWKV_PROJ])
    gates = _dot(xn, win_ref[:, CONV_PROJ + RWKV_PROJ:])

    gate_b = z_conv[:, :CONV_DIM]
    u = z_conv[:, CONV_DIM:2 * CONV_DIM] * z_conv[:, 2 * CONV_DIM:]
    prev_u = u_carry[...]
    u1 = _shift_rows(u, prev_u, 1)
    u2 = _shift_rows(u, prev_u, 2)
    u_carry[...] = u[-SUBLANES:]
    cw = cw_ref[...]
    conv = cw[0:1] * u2 + cw[1:2] * u1 + cw[2:3] * u
    ya = (gate_b * (conv + cb_ref[...])).astype(BF16)
    y_a = _dot(ya, wa_ref[...])
    ma_ref[0] = jax.nn.sigmoid(gates[:, :D_MODEL]) * y_a
    sgb_ref[0] = jax.nn.sigmoid(gates[:, D_MODEL:])

    z_prev = _shift_rows(z_rwkv, z_carry[...], 1)
    z_carry[...] = z_rwkv[-SUBLANES:]
    zs = z_rwkv + mu_ref[...] * (z_prev - z_rwkv)
    r = zs[:, :RWKV_DIM]
    k = zs[:, RWKV_DIM:2 * RWKV_DIM]
    v = zs[:, 2 * RWKV_DIM:3 * RWKV_DIM]
    wa_d = zs[:, 3 * RWKV_DIM:3 * RWKV_DIM + DECAY_RANK + ICLR_RANK]
    gd = zs[:, 3 * RWKV_DIM + DECAY_RANK + ICLR_RANK:]
    d = w0_ref[...] + _dot(jnp.tanh(wa_d), wup_ref[...], HI)
    lw_ref[0] = -DECAY_SCALE * jax.nn.sigmoid(d)
    a = jax.nn.sigmoid(a0_ref[...] + _dot(wa_d, aup_ref[...], HI))
    g_ref[0] = _dot(jax.nn.sigmoid(gd), gup_ref[...], HI)
    ones = _head_block_ones()
    kk = k * kk_ref[...]
    kkn_ref[0] = kk * lax.rsqrt(_dot(kk * kk, ones, HI) + 1e-12)
    k2 = k * (1.0 + (a - 1.0) * ka_ref[...])
    bonus_ref[0] = _dot(r * k2 * rk_ref[...], ones, HI) * v
    r_ref[0] = r
    k_ref[0] = k2
    v_ref[0] = v
    a_ref[0] = a


def _mix_in(x, ng, win, cw, cb, mu, w0, wup, a0, aup, gup, kk, ka, rk, wa, tb):
    bsz, seq, _ = x.shape
    full = lambda arr: pl.BlockSpec(arr.shape, lambda b, t: (0,) * arr.ndim)
    tok = lambda c: pl.BlockSpec((1, tb, c), lambda b, t: (b, t, 0))
    params = (ng, win, cw, cb, mu, w0, wup, a0, aup, gup, kk, ka, rk, wa)
    out_dims = (D_MODEL, D_MODEL) + (RWKV_DIM,) * 8
    return pl.pallas_call(
        _mix_in_kernel,
        grid=(bsz, seq // tb),
        in_specs=[tok(D_MODEL)] + [full(p) for p in params],
        out_specs=[tok(c) for c in out_dims],
        out_shape=[jax.ShapeDtypeStruct((bsz, seq, c), F32) for c in out_dims],
        scratch_shapes=[pltpu.VMEM((SUBLANES, CONV_DIM), F32), pltpu.VMEM((SUBLANES, RWKV_PROJ), F32)],
        compiler_params=pltpu.CompilerParams(dimension_semantics=("arbitrary", "arbitrary"),
                                             vmem_limit_bytes=VMEM_LIMIT),
        name="mix_in",
    )(x, *params)


RW_CHUNK = 64
RW_GROUP = 4
RW_LANES = RW_GROUP * HEAD_DIM
RW_ROWS = RW_GROUP * RW_CHUNK
RW_NGROUPS = RWKV_HEADS // RW_GROUP
RW_BATCH = 2
assert RW_CHUNK == HEAD_DIM


def _each(fn, *lists):
    return [fn(*xs) for xs in zip(*lists)]


def _rwkv_chunks(r, lw, k, v, kk, a, s, same, strict, incl, eye, tri):
    lc, n = RW_CHUNK, RW_ROWS
    bf = lambda t: t.astype(BF16)
    tile = lambda t: jnp.concatenate([t] * RW_GROUP, axis=0)
    stack = lambda t: bf(jnp.where(same, tile(t), 0.0))
    cum = _each(lambda t: _dot(tri, t, HI), lw)
    last = _each(lambda c: c[lc - 1:lc], cum)
    q = _each(lambda x, y: x * y, a, kk)
    pr_s = _each(lambda kk_, r_, c, lw_: jnp.concatenate([stack(-kk_ * jnp.exp(c - lw_)), stack(r_ * jnp.exp(c))],
                                                         axis=0), kk, r, cum, lw)
    qk_t = _each(lambda q_, k_, c: bf(jnp.concatenate([tile(q_ * jnp.exp(-c)), tile(k_ * jnp.exp(-c))], axis=0)),
                 q, k, cum)
    v_s = _each(stack, v)
    x = _each(_dot_nt, pr_s, qk_t)
    z0 = _each(lambda p, s_: _dot_nt(p, bf(s_)), pr_s, s)
    mpq = _each(lambda t: jnp.where(strict, t[:n, :n], 0.0), x)
    mpk = _each(lambda t: bf(jnp.where(strict, t[:n, n:], 0.0)), x)
    a_rqk = _each(lambda t: bf(jnp.concatenate([jnp.where(incl, t[n:, :n], 0.0), jnp.where(incl, t[n:, n:], 0.0)],
                                               axis=1)), x)
    rhs = _each(lambda z, m, v_: z[:n] + _dot(m, v_), z0, mpk, v_s)

    pw = _each(bf, mpq)
    tm = _each(lambda m: eye + m, mpq)
    pw = _each(lambda p: _dot(p, p), pw)
    for _ in range(int(np.log2(lc)) - 2):
        both = _each(lambda p, t: _dot(jnp.concatenate([bf(p), bf(t)], axis=0), bf(p)), pw, tm)
        pw = _each(lambda t: t[:n], both)
        tm = _each(lambda t, bt: t + bt[n:], tm, both)
    tm = _each(lambda t, p: t + _dot(bf(t), bf(p)), tm, pw)

    u = _each(lambda t, z: _dot(bf(t), bf(z)), tm, rhs)
    y_s = _each(lambda z, a_, u_, v_: z[n:] + _dot(a_, jnp.concatenate([bf(u_), v_], axis=0)), z0, a_rqk, u, v_s)
    y = _each(lambda t: sum(t[g * lc:(g + 1) * lc] for g in range(1, RW_GROUP)) + t[:lc], y_s)
    uv_t = _each(lambda u_, v_: jnp.concatenate([bf(u_.T), bf(v_.astype(F32).T)], axis=1), u, v_s)
    qk_end = _each(lambda q_, k_, c, l: jnp.concatenate([stack(q_ * jnp.exp(l - c)), stack(k_ * jnp.exp(l - c))],
                                                        axis=0), q, k, cum, last)
    s_new = _each(lambda l, s_, ut, qe: jnp.exp(l) * s_ + _dot(ut, qe), last, s, uv_t, qk_end)
    return y, s_new


def _rwkv_kernel(r_ref, lw_ref, k_ref, v_ref, kk_ref, a_ref, y_ref, s_ref):
    n, lc = RW_ROWS, RW_CHUNK

    @pl.when(pl.program_id(1) == 0)
    def _():
        s_ref[...] = jnp.zeros_like(s_ref)

    row = lax.broadcasted_iota(jnp.int32, (n, n), 0)
    col = lax.broadcasted_iota(jnp.int32, (n, n), 1)
    same = (row // lc) == (col // lc)
    strict = same & (col < row)
    incl = same & (col <= row)
    eye = jnp.where(row == col, 1.0, 0.0)
    tri = (lax.broadcasted_iota(jnp.int32, (lc, lc), 1) <= lax.broadcasted_iota(jnp.int32, (lc, lc), 0)).astype(F32)
    chains = [(b, slice(g * RW_LANES, (g + 1) * RW_LANES)) for b in range(RW_BATCH) for g in range(RW_NGROUPS)]
    args = [[ref[b, :, lanes] for b, lanes in chains] for ref in (r_ref, lw_ref, k_ref, v_ref, kk_ref, a_ref)]
    y, s_new = _rwkv_chunks(*args, [s_ref[c] for c in range(len(chains))], same, strict, incl, eye, tri)
    for c, (b, lanes) in enumerate(chains):
        y_ref[b, :, lanes] = y[c]
        s_ref[c] = s_new[c]


def _rwkv_scan(r, lw, k, v, kk, a):
    bsz, seq, _ = r.shape
    spec = pl.BlockSpec((RW_BATCH, RW_CHUNK, RWKV_DIM), lambda b, c: (b, c, 0))
    return pl.pallas_call(
        _rwkv_kernel,
        grid=(bsz // RW_BATCH, seq // RW_CHUNK),
        in_specs=[spec] * 6,
        out_specs=spec,
        out_shape=jax.ShapeDtypeStruct((bsz, seq, RWKV_DIM), F32),
        scratch_shapes=[pltpu.VMEM((RW_BATCH * RW_NGROUPS, RW_ROWS, RW_LANES), F32)],
        compiler_params=pltpu.CompilerParams(dimension_semantics=("arbitrary", "arbitrary"),
                                             vmem_limit_bytes=VMEM_LIMIT),
        name="rwkv_scan",
    )(r, lw, k, v, kk, a)


def _mix_out_kernel(x_ref, y_ref, bonus_ref, g_ref, ma_ref, sgb_ref, lng_ref, lnb_ref, wb_ref, wo_ref, h_ref):
    y = y_ref[...]
    mean_mat = _head_block_ones() * (1.0 / HEAD_DIM)
    d = y - _dot(y, mean_mat, HI)
    var = _dot(d * d, mean_mat, HI)
    yn = d * lax.rsqrt(var + GN_EPS) * lng_ref[...] + lnb_ref[...]
    yb = ((yn + bonus_ref[...]) * g_ref[...]).astype(BF16)
    merged = ma_ref[...] + sgb_ref[...] * _dot(yb, wb_ref[...])
    h_ref[...] = x_ref[...] + _dot(merged.astype(BF16), wo_ref[...])


def _mix_out(x, y, bonus, g, ma, sgb, lng, lnb, wb, wo, tb):
    ntok = x.shape[0]
    full = lambda arr: pl.BlockSpec(arr.shape, lambda t: (0,) * arr.ndim)
    tok = lambda c: pl.BlockSpec((tb, c), lambda t: (t, 0))
    return pl.pallas_call(
        _mix_out_kernel,
        grid=(ntok // tb,),
        in_specs=[tok(D_MODEL)] + [tok(RWKV_DIM)] * 3 + [tok(D_MODEL)] * 2 + [full(p) for p in (lng, lnb, wb, wo)],
        out_specs=tok(D_MODEL),
        out_shape=jax.ShapeDtypeStruct((ntok, D_MODEL), F32),
        compiler_params=pltpu.CompilerParams(dimension_semantics=("arbitrary",), vmem_limit_bytes=VMEM_LIMIT),
        name="mix_out",
    )(x, y, bonus, g, ma, sgb, lng, lnb, wb, wo)


def _sort16_pairs():
    n, pairs, p = PEER_TOPK, [], 1
    while p < n:
        k = p
        while k >= 1:
            for j in range(k % p, n - k, 2 * k):
                for i in range(min(k, n - j - k)):
                    if (i + j) // (2 * p) == (i + j + k) // (2 * p):
                        pairs.append((i + j, i + j + k))
            k //= 2
        p *= 2
    return pairs


def _vmax(a, b):
    return b if a is None else a if b is None else jnp.maximum(a, b)


def _vmin(a, b):
    return None if a is None or b is None else jnp.minimum(a, b)


def _merge_top16(a, b):
    pad = lambda t: list(t) + [None] * (PEER_TOPK - len(t))
    a, b = pad(a), pad(b)
    c = [_vmax(a[i], b[PEER_TOPK - 1 - i]) for i in range(PEER_TOPK)]
    d = PEER_TOPK // 2
    while d >= 1:
        for i in range(PEER_TOPK):
            if i & d == 0:
                c[i], c[i + d] = _vmax(c[i], c[i + d]), _vmin(c[i], c[i + d])
        d //= 2
    return [t for t in c if t is not None]


def _top16_sorted(vals):
    groups = []
    for g0 in range(0, len(vals), PEER_TOPK):
        grp = list(vals[g0:g0 + PEER_TOPK])
        for i, j in _sort16_pairs():
            grp[i], grp[j] = jnp.maximum(grp[i], grp[j]), jnp.minimum(grp[i], grp[j])
        groups.append(grp)
    while len(groups) > 1:
        groups = [_merge_top16(groups[i], groups[i + 1]) for i in range(0, len(groups), 2)]
    return groups[0]


def _pair_top16_sorted(a, b):
    k = PEER_TOPK
    lists = [[a[r] + b[j] for r in range(k // (j + 1))] for j in range(k // 2)]
    lists.append([a[0] + b[j] for j in range(k // 2, k)])
    top = lists[0]
    for t in lists[1:]:
        top = _merge_top16(top, t)
    return top


def _peer_topk_kernel(h_ref, g_ref, wq_ref, sk_ref, r1_ref, e1_ref, c0_ref, e0_ref, xn_ref, sc_ref, stage_ref, rk_ref):
    ngrp = h_ref.shape[0] // LANES
    xn_ref[...] = _rms(h_ref[...], g_ref[...]).astype(BF16)

    def head_body(h, carry):
        qh = _dot(xn_ref[...], wq_ref[h])
        halves = []
        for c in range(2):
            st = _dot_nt(sk_ref[2 * h + c], qh[:, c * PEER_HALF:(c + 1) * PEER_HALF], HI)
            sc_ref[c] = st
            for g in range(ngrp):
                stage_ref[pl.ds(g, PEER_NKEYS, stride=ngrp), :] = st[:, g * LANES:(g + 1) * LANES]
            halves.append(_top16_sorted([stage_ref[pl.ds(key * ngrp, ngrp), :] for key in range(PEER_NKEYS)]))
        a, b = halves
        top = _pair_top16_sorted(a, b)
        z = jnp.ones_like(top[0])
        for t in top[1:]:
            z = z + jnp.exp(t - top[0])
        for r in range(PEER_TOPK):
            rk_ref[r] = b[r]
        rk_ref[PEER_TOPK] = top[PEER_TOPK - 1]
        rk_ref[PEER_TOPK + 1] = a[0] + jnp.log(z)
        for g in range(ngrp):
            lanes = slice(g * LANES, (g + 1) * LANES)
            s0 = sc_ref[0, :, lanes]
            s1 = sc_ref[1, :, lanes]
            tok_row = lambda idx: rk_ref[idx, g:g + 1, :]
            tau = tok_row(PEER_TOPK)
            rank1 = jnp.zeros_like(s1)
            count0 = jnp.zeros_like(s0)
            for r in range(PEER_TOPK):
                rank1 = rank1 + jnp.where(tok_row(r) > s1, 1.0, 0.0)
                count0 = count0 + jnp.where(s0 + tok_row(r) >= tau, 1.0, 0.0)
            r1_ref[h, g] = rank1.astype(BF16)
            c0_ref[h, g] = count0
            e1_ref[h, g] = jnp.exp(s1 - tok_row(0)).astype(BF16)
            e0_ref[h, g] = jnp.exp(s0 - tok_row(PEER_TOPK + 1))
        return carry

    lax.fori_loop(0, PEER_HEADS, head_body, 0)


def _peer_topk(h1, g, wq, sk, tb):
    ntok = h1.shape[0]
    ngrp = tb // LANES
    full = lambda arr: pl.BlockSpec(arr.shape, lambda t: (0,) * arr.ndim)
    key_spec = pl.BlockSpec((PEER_HEADS, ngrp, PEER_NKEYS, LANES), lambda t: (0, t, 0, 0))
    key_shape = lambda dt: jax.ShapeDtypeStruct((PEER_HEADS, ntok // LANES, PEER_NKEYS, LANES), dt)
    return pl.pallas_call(
        _peer_topk_kernel,
        grid=(ntok // tb,),
        in_specs=[pl.BlockSpec((tb, D_MODEL), lambda t: (t, 0)), full(g), full(wq), full(sk)],
        out_specs=[key_spec] * 4,
        out_shape=[key_shape(BF16), key_shape(BF16), key_shape(F32), key_shape(F32)],
        scratch_shapes=[pltpu.VMEM((tb, D_MODEL), BF16),
                        pltpu.VMEM((2, PEER_NKEYS, tb), F32),
                        pltpu.VMEM((PEER_NKEYS * ngrp, LANES), F32),
                        pltpu.VMEM((PEER_TOPK + 2, ngrp, LANES), F32)],
        compiler_params=pltpu.CompilerParams(dimension_semantics=("arbitrary",), vmem_limit_bytes=VMEM_LIMIT),
        name="peer_topk",
    )(h1, g, wq, sk)


def _gelu_tanh(x):
    return x * (0.5 * (1.0 + jnp.tanh(np.sqrt(2.0 / np.pi) * (x + 0.044715 * (x * x * x)))))


def _peer_ffn_kernel(hx_ref, ho_ref, g_ref, r1_ref, e1_ref, c0_ref, e0_ref, u_ref, vt_ref, o_ref,
                     xn_ref, acc_ref, ht_ref, p_ref, r1s_ref, e1s_ref, *, nchunk, nsteps):
    k = pl.program_id(0)
    ngrp = hx_ref.shape[0] // LANES
    nkey = u_ref.shape[0] // PEER_NKEYS
    e_gate = jnp.clip(k - 1, 0, nsteps - 1) % nchunk
    e_acc = jnp.clip(k - 2, 0, nsteps - 1) % nchunk

    @pl.when(k == 0)
    def _():
        ht_ref[...] = jnp.zeros_like(ht_ref)
        p_ref[...] = jnp.zeros_like(p_ref)

    @pl.when((k % nchunk == 0) & (k < nsteps))
    def _():
        xn_ref[...] = _rms(hx_ref[...], g_ref[...]).astype(BF16)

    @pl.when(e_acc == 0)
    def _():
        acc_ref[...] = jnp.zeros_like(acc_ref)

    @pl.when(e_gate == 0)
    def _():
        r1s_ref[...] = r1_ref[...]
        e1s_ref[...] = e1_ref[...]

    half = u_ref.shape[0] // 2

    def roles(cur, prev):
        def ht_part(n):
            rows = slice(n * half, (n + 1) * half)
            ht = _dot_nt(u_ref[rows, :], xn_ref[...])
            for g in range(ngrp):
                ht_ref[cur, g, rows, :] = ht[:, g * LANES:(g + 1) * LANES]

        def acc_part(n):
            rows = slice(n * half, (n + 1) * half)
            p = jnp.concatenate([p_ref[cur, g, rows, :] for g in range(ngrp)], axis=1)
            acc_ref[...] += _dot(vt_ref[:, rows], p)

        def gate_part(il):
            i = e_gate * nkey + il
            rows = slice(il * PEER_NKEYS, (il + 1) * PEER_NKEYS)
            for g in range(ngrp):
                w = jnp.zeros((PEER_NKEYS, LANES), BF16)
                for h in range(PEER_HEADS):
                    cnt = jnp.broadcast_to(c0_ref[h, g, pl.ds(i, 1), :].astype(BF16), (PEER_NKEYS, LANES))
                    e0 = jnp.broadcast_to(e0_ref[h, g, pl.ds(i, 1), :].astype(BF16), (PEER_NKEYS, LANES))
                    w = w + jnp.where(r1s_ref[h, g] < cnt, e1s_ref[h, g], 0.0) * e0
                p_ref[prev, g, rows, :] = _gelu_tanh(ht_ref[prev, g, rows, :]).astype(BF16) * w

        quarter = nkey // 4
        pieces = (functools.partial(ht_part, 0), functools.partial(acc_part, 0),
                  functools.partial(ht_part, 1), functools.partial(acc_part, 1))
        for n, mxu_piece in enumerate(pieces):
            mxu_piece()
            for il in range(n * quarter, (n + 1) * quarter):
                gate_part(il)

    for parity in range(2):
        pl.when(k % 2 == parity)(functools.partial(roles, parity, 1 - parity))

    @pl.when((e_acc == nchunk - 1) & (k >= 2))
    def _():
        o_ref[...] = ho_ref[...] + acc_ref[...].T


def _peer_ffn(h1, g, r1, e1, c0, e0, u, vt, tb, ec):
    ntok = h1.shape[0]
    ngrp = tb // LANES
    nchunk = u.shape[0] // ec
    nsteps = (ntok // tb) * nchunk
    last = nsteps - 1
    at = lambda shift: (lambda k: jnp.clip(k - shift, 0, last))
    tok_blk = lambda shift: (lambda k: (at(shift)(k) // nchunk, 0))
    key_spec = pl.BlockSpec((PEER_HEADS, ngrp, PEER_NKEYS, LANES), lambda k: (0, at(1)(k) // nchunk, 0, 0))
    return pl.pallas_call(
        functools.partial(_peer_ffn_kernel, nchunk=nchunk, nsteps=nsteps),
        grid=(nsteps + 2,),
        in_specs=[pl.BlockSpec((tb, D_MODEL), tok_blk(0)),
                  pl.BlockSpec((tb, D_MODEL), tok_blk(2)),
                  pl.BlockSpec(g.shape, lambda k: (0, 0)),
                  key_spec, key_spec, key_spec, key_spec,
                  pl.BlockSpec((ec, D_MODEL), lambda k: (at(0)(k) % nchunk, 0)),
                  pl.BlockSpec((D_MODEL, ec), lambda k: (0, at(2)(k) % nchunk))],
        out_specs=pl.BlockSpec((tb, D_MODEL), tok_blk(2)),
        out_shape=jax.ShapeDtypeStruct((ntok, D_MODEL), F32),
        scratch_shapes=[pltpu.VMEM((tb, D_MODEL), BF16),
                        pltpu.VMEM((D_MODEL, tb), F32),
                        pltpu.VMEM((2, ngrp, ec, LANES), F32),
                        pltpu.VMEM((2, ngrp, ec, LANES), BF16),
                        pltpu.VMEM((PEER_HEADS, ngrp, PEER_NKEYS, LANES), BF16),
                        pltpu.VMEM((PEER_HEADS, ngrp, PEER_NKEYS, LANES), BF16)],
        compiler_params=pltpu.CompilerParams(dimension_semantics=("arbitrary",), vmem_limit_bytes=VMEM_LIMIT),
        name="peer_ffn",
    )(h1, h1, g, r1, e1, c0, e0, u, vt)


def _ple_out_kernel(h_ref, p_ref, g_ref, wg_ref, wp_ref, fg_ref, o_ref, *, final):
    h = h_ref[...]
    gate = jax.nn.sigmoid(_dot(_rms(h, g_ref[...]).astype(BF16), wg_ref[...]))
    h = h + gate * _dot(p_ref[...].astype(BF16), wp_ref[...])
    o_ref[...] = _rms(h, fg_ref[...]) if final else h


def _ple_out(h, p, g, wg, wp, fg, tb, final):
    ntok = h.shape[0]
    full = lambda arr: pl.BlockSpec(arr.shape, lambda t: (0,) * arr.ndim)
    return pl.pallas_call(
        functools.partial(_ple_out_kernel, final=final),
        grid=(ntok // tb,),
        in_specs=[pl.BlockSpec((tb, D_MODEL), lambda t: (t, 0)), pl.BlockSpec((tb, PLE_DIM), lambda t: (t, 0)),
                  full(g), full(wg), full(wp), full(fg)],
        out_specs=pl.BlockSpec((tb, D_MODEL), lambda t: (t, 0)),
        out_shape=jax.ShapeDtypeStruct((ntok, D_MODEL), F32),
        compiler_params=pltpu.CompilerParams(dimension_semantics=("arbitrary",), vmem_limit_bytes=VMEM_LIMIT),
        name="ple_out",
    )(h, p, g, wg, wp, fg)


MIX_IN_TOKENS = 256
MIX_OUT_TOKENS = 512
TOPK_TOKENS = 1024
FFN_TOKENS = 512
FFN_EXPERTS = 1024
PLE_TOKENS = 512


def kernel(x, p, norm_mix_g, w_in, conv_w, conv_b, shift_mu, w0, w_up, a0, a_up, g_up, k_k, k_a, r_k, ln_x_g, ln_x_b,
           w_branch_a, w_branch_b, w_out, norm_ffn_g, peer_wq, peer_subkeys, peer_u, peer_v, norm_ple_g, ple_gate_w,
           ple_proj_w, final_norm_g):
    bsz, seq, _ = x.shape
    ntok = bsz * seq
    row = lambda t: t.reshape(1, -1)
    h = x
    for i in range(w_in.shape[0]):
        zeros_rank = jnp.zeros((DECAY_RANK, RWKV_DIM), F32)
        ma, sgb, r, lw, k, v, kk, a, g, bonus = _mix_in(
            h, row(norm_mix_g[i]), w_in[i].astype(BF16), conv_w[i], row(conv_b[i]), row(shift_mu[i]), row(w0[i]),
            jnp.concatenate([w_up[i], zeros_rank], axis=0), row(a0[i]),
            jnp.concatenate([zeros_rank, a_up[i]], axis=0), g_up[i], row(k_k[i]), row(k_a[i]), row(r_k[i]),
            w_branch_a[i].astype(BF16), MIX_IN_TOKENS)
        y = _rwkv_scan(r, lw, k, v, kk, a)
        flat = lambda t: t.reshape(ntok, t.shape[-1])
        h1 = _mix_out(flat(h), flat(y), flat(bonus), flat(g), flat(ma), flat(sgb), row(ln_x_g[i]), row(ln_x_b[i]),
                      w_branch_b[i].astype(BF16), w_out[i].astype(BF16), MIX_OUT_TOKENS)
        wq = peer_wq[i].astype(BF16).reshape(D_MODEL, PEER_HEADS, 2 * PEER_HALF).transpose(1, 0, 2)
        r1, e1, c0, e0 = _peer_topk(h1, row(norm_ffn_g[i]), wq,
                                    peer_subkeys[i].reshape(2 * PEER_HEADS, PEER_NKEYS, PEER_HALF), TOPK_TOKENS)
        h2 = _peer_ffn(h1, row(norm_ffn_g[i]), r1, e1, c0, e0, peer_u[i].astype(BF16),
                       peer_v[i].T.astype(BF16), FFN_TOKENS, FFN_EXPERTS)
        h = _ple_out(h2, flat(p[i]), row(norm_ple_g[i]), ple_gate_w[i].astype(BF16), ple_proj_w[i].astype(BF16),
                     row(final_norm_g), PLE_TOKENS, final=(i == w_in.shape[0] - 1))
        h = h.reshape(bsz, seq, D_MODEL)
    return h
```

```python
import functools

import jax
import jax.numpy as jnp
import numpy as np
from jax import lax
from jax.experimental import pallas as pl
from jax.experimental.pallas import tpu as pltpu

D_MODEL = 1024
CONV_DIM = 512
RWKV_HEADS = 8
HEAD_DIM = 64
RWKV_DIM = RWKV_HEADS * HEAD_DIM
DECAY_RANK = 64
ICLR_RANK = 64
GATE_RANK = 128
DECAY_SCALE = 0.6065306597126334
GN_EPS = 64e-5
NORM_EPS = 1e-6
CONV_PROJ = 3 * CONV_DIM
RWKV_PROJ = 3 * RWKV_DIM + DECAY_RANK + ICLR_RANK + GATE_RANK
PEER_HEADS = 8
PEER_NKEYS = 128
PEER_HALF = 128
PEER_TOPK = 16
PLE_DIM = 256

LANES = 128
SUBLANES = 8
VMEM_LIMIT = 56 * 1024 * 1024

HI = lax.Precision.HIGHEST
F32 = jnp.float32
BF16 = jnp.bfloat16


def _dot(a, b, precision=None):
    return jnp.dot(a, b, preferred_element_type=F32, precision=precision)


def _dot_nt(a, b, precision=None):
    return lax.dot_general(a, b, (((1,), (1,)), ((), ())), preferred_element_type=F32, precision=precision)


def _dot_tn(a, b, precision=None):
    return lax.dot_general(a, b, (((0,), (0,)), ((), ())), preferred_element_type=F32, precision=precision)


def _rms(x, g):
    return x * lax.rsqrt(jnp.mean(x * x, axis=-1, keepdims=True) + NORM_EPS) * g


def _shift_rows(x, prev_rows, n):
    rolled = pltpu.roll(x, n, axis=0)
    head = pltpu.roll(prev_rows, n, axis=0)
    row = lax.broadcasted_iota(jnp.int32, x.shape, 0)
    head_full = jnp.concatenate([head, rolled[SUBLANES:]], axis=0)
    return jnp.where(row < n, head_full, rolled)


def _head_block_ones():
    r = lax.broadcasted_iota(jnp.int32, (RWKV_DIM, RWKV_DIM), 0) // HEAD_DIM
    c = lax.broadcasted_iota(jnp.int32, (RWKV_DIM, RWKV_DIM), 1) // HEAD_DIM
    return (r == c).astype(F32)


def _mix_in_kernel(x_ref, ng_ref, win_ref, cw_ref, cb_ref, mu_ref, w0_ref, wup_ref, a0_ref, aup_ref, gup_ref,
                   kk_ref, ka_ref, rk_ref, wa_ref,
                   ma_ref, sgb_ref, r_ref, lw_ref, k_ref, v_ref, kkn_ref, a_ref, g_ref, bonus_ref,
                   u_carry, z_carry):
    t_blk = pl.program_id(1)

    @pl.when(t_blk == 0)
    def _():
        u_carry[...] = jnp.zeros_like(u_carry)
        z_carry[...] = jnp.zeros_like(z_carry)

    x = x_ref[0]
    xn = _rms(x, ng_ref[...]).astype(BF16)
    z_conv = _dot(xn, win_ref[:, :CONV_PROJ])
    z_rwkv = _dot(xn, win_ref[:, CONV_PROJ:CONV_PROJ + RWKV_PROJ])
    gates = _dot(xn, win_ref[:, CONV_PROJ + RWKV_PROJ:])

    gate_b = z_conv[:, :CONV_DIM]
    u = z_conv[:, CONV_DIM:2 * CONV_DIM] * z_conv[:, 2 * CONV_DIM:]
    prev_u = u_carry[...]
    u1 = _shift_rows(u, prev_u, 1)
    u2 = _shift_rows(u, prev_u, 2)
    u_carry[...] = u[-SUBLANES:]
    cw = cw_ref[...]
    conv = cw[0:1] * u2 + cw[1:2] * u1 + cw[2:3] * u
    ya = (gate_b * (conv + cb_ref[...])).astype(BF16)
    y_a = _dot(ya, wa_ref[...])
    ma_ref[0] = jax.nn.sigmoid(gates[:, :D_MODEL]) * y_a
    sgb_ref[0] = jax.nn.sigmoid(gates[:, D_MODEL:])

    z_prev = _shift_rows(z_rwkv, z_carry[...], 1)
    z_carry[...] = z_rwkv[-SUBLANES:]
    zs = z_rwkv + mu_ref[...] * (z_prev - z_rwkv)
    r = zs[:, :RWKV_DIM]
    k = zs[:, RWKV_DIM:2 * RWKV_DIM]
    v = zs[:, 2 * RWKV_DIM:3 * RWKV_DIM]
    wa_d = zs[:, 3 * RWKV_DIM:3 * RWKV_DIM + DECAY_RANK + ICLR_RANK]
    gd = zs[:, 3 * RWKV_DIM + DECAY_RANK + ICLR_RANK:]
    d = w0_ref[...] + _dot(jnp.tanh(wa_d), wup_ref[...], HI)
    lw_ref[0] = -DECAY_SCALE * jax.nn.sigmoid(d)
    a = jax.nn.sigmoid(a0_ref[...] + _dot(wa_d, aup_ref[...], HI))
    g_ref[0] = _dot(jax.nn.sigmoid(gd), gup_ref[...], HI)
    ones = _head_block_ones()
    kk = k * kk_ref[...]
    kkn_ref[0] = kk * lax.rsqrt(_dot(kk * kk, ones, HI) + 1e-12)
    k2 = k * (1.0 + (a - 1.0) * ka_ref[...])
    bonus_ref[0] = _dot(r * k2 * rk_ref[...], ones, HI) * v
    r_ref[0] = r
    k_ref[0] = k2
    v_ref[0] = v
    a_ref[0] = a


def _mix_in(x, ng, win, cw, cb, mu, w0, wup, a0, aup, gup, kk, ka, rk, wa, tb):
    bsz, seq, _ = x.shape
    full = lambda arr: pl.BlockSpec(arr.shape, lambda b, t: (0,) * arr.ndim)
    tok = lambda c: pl.BlockSpec((1, tb, c), lambda b, t: (b, t, 0))
    params = (ng, win, cw, cb, mu, w0, wup, a0, aup, gup, kk, ka, rk, wa)
    out_dims = (D_MODEL, D_MODEL) + (RWKV_DIM,) * 8
    return pl.pallas_call(
        _mix_in_kernel,
        grid=(bsz, seq // tb),
        in_specs=[tok(D_MODEL)] + [full(p) for p in params],
        out_specs=[tok(c) for c in out_dims],
        out_shape=[jax.ShapeDtypeStruct((bsz, seq, c), F32) for c in out_dims],
        scratch_shapes=[pltpu.VMEM((SUBLANES, CONV_DIM), F32), pltpu.VMEM((SUBLANES, RWKV_PROJ), F32)],
        compiler_params=pltpu.CompilerParams(dimension_semantics=("arbitrary", "arbitrary"),
                                             vmem_limit_bytes=VMEM_LIMIT),
        name="mix_in",
    )(x, *params)


RW_CHUNK = 64
RW_GROUP = 4
RW_LANES = RW_GROUP * HEAD_DIM
RW_ROWS = RW_GROUP * RW_CHUNK
RW_NGROUPS = RWKV_HEADS // RW_GROUP
RW_BATCH = 2
assert RW_CHUNK == HEAD_DIM


def _each(fn, *lists):
    return [fn(*xs) for xs in zip(*lists)]


def _rwkv_chunks(r, lw, k, v, kk, a, s, same, strict, incl, eye, tri):
    lc, n = RW_CHUNK, RW_ROWS
    bf = lambda t: t.astype(BF16)
    tile = lambda t: jnp.concatenate([t] * RW_GROUP, axis=0)
    stack = lambda t: bf(jnp.where(same, tile(t), 0.0))
    cum = _each(lambda t: _dot(tri, t, HI), lw)
    last = _each(lambda c: c[lc - 1:lc], cum)
    q = _each(lambda x, y: x * y, a, kk)
    pr_s = _each(lambda kk_, r_, c, lw_: jnp.concatenate([stack(-kk_ * jnp.exp(c - lw_)), stack(r_ * jnp.exp(c))],
                                                         axis=0), kk, r, cum, lw)
    qk_t = _each(lambda q_, k_, c: bf(jnp.concatenate([tile(q_ * jnp.exp(-c)), tile(k_ * jnp.exp(-c))], axis=0)),
                 q, k, cum)
    v_s = _each(stack, v)
    x = _each(_dot_nt, pr_s, qk_t)
    z0 = _each(lambda p, s_: _dot_nt(p, bf(s_)), pr_s, s)
    mpq = _each(lambda t: jnp.where(strict, t[:n, :n], 0.0), x)
    mpk = _each(lambda t: bf(jnp.where(strict, t[:n, n:], 0.0)), x)
    a_rqk = _each(lambda t: bf(jnp.concatenate([jnp.where(incl, t[n:, :n], 0.0), jnp.where(incl, t[n:, n:], 0.0)],
                                               axis=1)), x)
    rhs = _each(lambda z, m, v_: z[:n] + _dot(m, v_), z0, mpk, v_s)

    pw = _each(bf, mpq)
    tm = _each(lambda m: eye + m, mpq)
    pw = _each(lambda p: _dot(p, p), pw)
    for _ in range(int(np.log2(lc)) - 2):
        both = _each(lambda p, t: _dot(jnp.concatenate([bf(p), bf(t)], axis=0), bf(p)), pw, tm)
        pw = _each(lambda t: t[:n], both)
        tm = _each(lambda t, bt: t + bt[n:], tm, both)
    tm = _each(lambda t, p: t + _dot(bf(t), bf(p)), tm, pw)

    u = _each(lambda t, z: _dot(bf(t), bf(z)), tm, rhs)
    y_s = _each(lambda z, a_, u_, v_: z[n:] + _dot(a_, jnp.concatenate([bf(u_), v_], axis=0)), z0, a_rqk, u, v_s)
    y = _each(lambda t: sum(t[g * lc:(g + 1) * lc] for g in range(1, RW_GROUP)) + t[:lc], y_s)
    uv_t = _each(lambda u_, v_: jnp.concatenate([bf(u_.T), bf(v_.astype(F32).T)], axis=1), u, v_s)
    qk_end = _each(lambda q_, k_, c, l: jnp.concatenate([stack(q_ * jnp.exp(l - c)), stack(k_ * jnp.exp(l - c))],
                                                        axis=0), q, k, cum, last)
    s_new = _each(lambda l, s_, ut, qe: jnp.exp(l) * s_ + _dot(ut, qe), last, s, uv_t, qk_end)
    return y, s_new


def _rwkv_kernel(r_ref, lw_ref, k_ref, v_ref, kk_ref, a_ref, y_ref, s_ref):
    n, lc = RW_ROWS, RW_CHUNK

    @pl.when(pl.program_id(1) == 0)
    def _():
        s_ref[...] = jnp.zeros_like(s_ref)

    row = lax.broadcasted_iota(jnp.int32, (n, n), 0)
    col = lax.broadcasted_iota(jnp.int32, (n, n), 1)
    same = (row // lc) == (col // lc)
    strict = same & (col < row)
    incl = same & (col <= row)
    eye = jnp.where(row == col, 1.0, 0.0)
    tri = (lax.broadcasted_iota(jnp.int32, (lc, lc), 1) <= lax.broadcasted_iota(jnp.int32, (lc, lc), 0)).astype(F32)
    chains = [(b, slice(g * RW_LANES, (g + 1) * RW_LANES)) for b in range(RW_BATCH) for g in range(RW_NGROUPS)]
    args = [[ref[b, :, lanes] for b, lanes in chains] for ref in (r_ref, lw_ref, k_ref, v_ref, kk_ref, a_ref)]
    y, s_new = _rwkv_chunks(*args, [s_ref[c] for c in range(len(chains))], same, strict, incl, eye, tri)
    for c, (b, lanes) in enumerate(chains):
        y_ref[b, :, lanes] = y[c]
        s_ref[c] = s_new[c]


def _rwkv_scan(r, lw, k, v, kk, a):
    bsz, seq, _ = r.shape
    spec = pl.BlockSpec((RW_BATCH, RW_CHUNK, RWKV_DIM), lambda b, c: (b, c, 0))
    return pl.pallas_call(
        _rwkv_kernel,
        grid=(bsz // RW_BATCH, seq // RW_CHUNK),
        in_specs=[spec] * 6,
        out_specs=spec,
        out_shape=jax.ShapeDtypeStruct((bsz, seq, RWKV_DIM), F32),
        scratch_shapes=[pltpu.VMEM((RW_BATCH * RW_NGROUPS, RW_ROWS, RW_LANES), F32)],
        compiler_params=pltpu.CompilerParams(dimension_semantics=("arbitrary", "arbitrary"),
                                             vmem_limit_bytes=VMEM_LIMIT),
        name="rwkv_scan",
    )(r, lw, k, v, kk, a)


def _mix_out_kernel(x_ref, y_ref, bonus_ref, g_ref, ma_ref, sgb_ref, lng_ref, lnb_ref, wb_ref, wo_ref, h_ref):
    y = y_ref[...]
    mean_mat = _head_block_ones() * (1.0 / HEAD_DIM)
    d = y - _dot(y, mean_mat, HI)
    var = _dot(d * d, mean_mat, HI)
    yn = d * lax.rsqrt(var + GN_EPS) * lng_ref[...] + lnb_ref[...]
    yb = ((yn + bonus_ref[...]) * g_ref[...]).astype(BF16)
    merged = ma_ref[...] + sgb_ref[...] * _dot(yb, wb_ref[...])
    h_ref[...] = x_ref[...] + _dot(merged.astype(BF16), wo_ref[...])


def _mix_out(x, y, bonus, g, ma, sgb, lng, lnb, wb, wo, tb):
    ntok = x.shape[0]
    full = lambda arr: pl.BlockSpec(arr.shape, lambda t: (0,) * arr.ndim)
    tok = lambda c: pl.BlockSpec((tb, c), lambda t: (t, 0))
    return pl.pallas_call(
        _mix_out_kernel,
        grid=(ntok // tb,),
        in_specs=[tok(D_MODEL)] + [tok(RWKV_DIM)] * 3 + [tok(D_MODEL)] * 2 + [full(p) for p in (lng, lnb, wb, wo)],
        out_specs=tok(D_MODEL),
        out_shape=jax.ShapeDtypeStruct((ntok, D_MODEL), F32),
        compiler_params=pltpu.CompilerParams(dimension_semantics=("arbitrary",), vmem_limit_bytes=VMEM_LIMIT),
        name="mix_out",
    )(x, y, bonus, g, ma, sgb, lng, lnb, wb, wo)


def _sort16_pairs():
    n, pairs, p = PEER_TOPK, [], 1
    while p < n:
        k = p
        while k >= 1:
            for j in range(k % p, n - k, 2 * k):
                for i in range(min(k, n - j - k)):
                    if (i + j) // (2 * p) == (i + j + k) // (2 * p):
                        pairs.append((i + j, i + j + k))
            k //= 2
        p *= 2
    return pairs


def _vmax(a, b):
    return b if a is None else a if b is None else jnp.maximum(a, b)


def _vmin(a, b):
    return None if a is None or b is None else jnp.minimum(a, b)


def _merge_top16(a, b):
    pad = lambda t: list(t) + [None] * (PEER_TOPK - len(t))
    a, b = pad(a), pad(b)
    c = [_vmax(a[i], b[PEER_TOPK - 1 - i]) for i in range(PEER_TOPK)]
    d = PEER_TOPK // 2
    while d >= 1:
        for i in range(PEER_TOPK):
            if i & d == 0:
                c[i], c[i + d] = _vmax(c[i], c[i + d]), _vmin(c[i], c[i + d])
        d //= 2
    return [t for t in c if t is not None]


def _top16_sorted(vals):
    groups = []
    for g0 in range(0, len(vals), PEER_TOPK):
        grp = list(vals[g0:g0 + PEER_TOPK])
        for i, j in _sort16_pairs():
            grp[i], grp[j] = jnp.maximum(grp[i], grp[j]), jnp.minimum(grp[i], grp[j])
        groups.append(grp)
    while len(groups) > 1:
        groups = [_merge_top16(groups[i], groups[i + 1]) for i in range(0, len(groups), 2)]
    return groups[0]


def _pair_top16_sorted(a, b):
    k = PEER_TOPK
    lists = [[a[r] + b[j] for r in range(k // (j + 1))] for j in range(k // 2)]
    lists.append([a[0] + b[j] for j in range(k // 2, k)])
    top = lists[0]
    for t in lists[1:]:
        top = _merge_top16(top, t)
    return top


def _peer_topk_kernel(h_ref, g_ref, wq_ref, sk_ref, r1_ref, e1_ref, c0_ref, e0_ref, xn_ref, sc_ref, stage_ref, rk_ref):
    ngrp = h_ref.shape[0] // LANES
    xn_ref[...] = _rms(h_ref[...], g_ref[...]).astype(BF16)

    def head_body(h, carry):
        qh = _dot(xn_ref[...], wq_ref[h])
        halves = []
        for c in range(2):
            st = _dot_nt(sk_ref[2 * h + c], qh[:, c * PEER_HALF:(c + 1) * PEER_HALF], HI)
            sc_ref[c] = st
            for g in range(ngrp):
                stage_ref[pl.ds(g, PEER_NKEYS, stride=ngrp), :] = st[:, g * LANES:(g + 1) * LANES]
            halves.append(_top16_sorted([stage_ref[pl.ds(key * ngrp, ngrp), :] for key in range(PEER_NKEYS)]))
        a, b = halves
        top = _pair_top16_sorted(a, b)
        z = jnp.ones_like(top[0])
        for t in top[1:]:
            z = z + jnp.exp(t - top[0])
        for r in range(PEER_TOPK):
            rk_ref[r] = b[r]
        rk_ref[PEER_TOPK] = top[PEER_TOPK - 1]
        rk_ref[PEER_TOPK + 1] = a[0] + jnp.log(z)
        for g in range(ngrp):
            lanes = slice(g * LANES, (g + 1) * LANES)
            s0 = sc_ref[0, :, lanes]
            s1 = sc_ref[1, :, lanes]
            tok_row = lambda idx: rk_ref[idx, g:g + 1, :]
            tau = tok_row(PEER_TOPK)
            rank1 = jnp.zeros_like(s1)
            count0 = jnp.zeros_like(s0)
            for r in range(PEER_TOPK):
                rank1 = rank1 + jnp.where(tok_row(r) > s1, 1.0, 0.0)
                count0 = count0 + jnp.where(s0 + tok_row(r) >= tau, 1.0, 0.0)
            r1_ref[h, g] = rank1.astype(BF16)
            c0_ref[h, g] = count0
            e1_ref[h, g] = jnp.exp(s1 - tok_row(0)).astype(BF16)
            e0_ref[h, g] = jnp.exp(s0 - tok_row(PEER_TOPK + 1))
        return carry

    lax.fori_loop(0, PEER_HEADS, head_body, 0)


def _peer_topk(h1, g, wq, sk, tb):
    ntok = h1.shape[0]
    ngrp = tb // LANES
    full = lambda arr: pl.BlockSpec(arr.shape, lambda t: (0,) * arr.ndim)
    key_spec = pl.BlockSpec((PEER_HEADS, ngrp, PEER_NKEYS, LANES), lambda t: (0, t, 0, 0))
    key_shape = lambda dt: jax.ShapeDtypeStruct((PEER_HEADS, ntok // LANES, PEER_NKEYS, LANES), dt)
    return pl.pallas_call(
        _peer_topk_kernel,
        grid=(ntok // tb,),
        in_specs=[pl.BlockSpec((tb, D_MODEL), lambda t: (t, 0)), full(g), full(wq), full(sk)],
        out_specs=[key_spec] * 4,
        out_shape=[key_shape(BF16), key_shape(BF16), key_shape(F32), key_shape(F32)],
        scratch_shapes=[pltpu.VMEM((tb, D_MODEL), BF16),
                        pltpu.VMEM((2, PEER_NKEYS, tb), F32),
                        pltpu.VMEM((PEER_NKEYS * ngrp, LANES), F32),
                        pltpu.VMEM((PEER_TOPK + 2, ngrp, LANES), F32)],
        compiler_params=pltpu.CompilerParams(dimension_semantics=("arbitrary",), vmem_limit_bytes=VMEM_LIMIT),
        name="peer_topk",
    )(h1, g, wq, sk)


GATE_KEYS = 64
GATE_FIRST = 4


def _gelu_tanh(x):
    return x * (0.5 * (1.0 + jnp.tanh(np.sqrt(2.0 / np.pi) * (x + 0.044715 * (x * x * x)))))


def _peer_ffn_kernel(hx_ref, ho_ref, g_ref, r1_ref, e1_ref, c0_ref, e0_ref, u_ref, vt_ref, o_ref,
                     xn_ref, acc_ref, ht0_ref, ht1_ref, p0_ref, p1_ref, r1s_ref, e1s_ref, *, nchunk, nsteps):
    k = pl.program_id(0)
    ngrp = hx_ref.shape[0] // LANES
    nkey = u_ref.shape[0] // PEER_NKEYS
    e_gate = jnp.clip(k - 1, 0, nsteps - 1) % nchunk
    e_acc = jnp.clip(k - 2, 0, nsteps - 1) % nchunk

    @pl.when(k == 0)
    def _():
        for ref in (ht0_ref, ht1_ref, p0_ref, p1_ref):
            ref[...] = jnp.zeros_like(ref)

    @pl.when((k % nchunk == 0) & (k < nsteps))
    def _():
        xn_ref[...] = _rms(hx_ref[...], g_ref[...]).T.astype(BF16)

    @pl.when(e_acc == 0)
    def _():
        acc_ref[...] = jnp.zeros_like(acc_ref)

    @pl.when(e_gate == 0)
    def _():
        r1s_ref[...] = r1_ref[...].astype(F32)
        e1s_ref[...] = e1_ref[...].astype(F32)

    piece = 2 * PEER_NKEYS

    def roles(ht_new, ht_old, p_new, p_old):
        def ht_unit(n, tt):
            rows = slice(n * piece, (n + 1) * piece)
            ht = _dot(u_ref[rows, :], xn_ref[:, tt * piece:(tt + 1) * piece])
            for gl in range(piece // LANES):
                ht_new[tt * (piece // LANES) + gl, rows, :] = ht[:, gl * LANES:(gl + 1) * LANES]

        def acc_unit(n, tt):
            rows = slice(n * piece, (n + 1) * piece)
            p = jnp.concatenate([p_old[tt * (piece // LANES) + gl, rows, :] for gl in range(piece // LANES)], axis=1)
            acc_ref[:, tt * piece:(tt + 1) * piece] += _dot(vt_ref[:, rows], p)

        def gate_unit(ils, g, jh):
            keys = slice(jh * GATE_KEYS, (jh + 1) * GATE_KEYS)
            w = [jnp.zeros((GATE_KEYS, LANES), F32) for _ in ils]
            for h in range(PEER_HEADS):
                r1, e1 = r1s_ref[h, g, keys, :], e1s_ref[h, g, keys, :]
                for n, il in enumerate(ils):
                    i = e_gate * nkey + il
                    w[n] = w[n] + jnp.where(r1 < c0_ref[h, g, pl.ds(i, 1), :], e1, 0.0) * e0_ref[h, g, pl.ds(i, 1), :]
            for n, il in enumerate(ils):
                rows = slice(il * PEER_NKEYS + jh * GATE_KEYS, il * PEER_NKEYS + (jh + 1) * GATE_KEYS)
                p_new[g, rows, :] = (_gelu_tanh(ht_old[g, rows, :]) * w[n]).astype(BF16)

        npiece, ntt = u_ref.shape[0] // piece, ngrp * LANES // piece
        mxu_units = [functools.partial(unit, n, tt) for n in range(npiece) for tt in range(ntt)
                     for unit in (ht_unit, acc_unit)]
        gate_units = [(tuple(range(i0, i0 + GATE_FIRST)), g, jh) for i0 in range(0, nkey, GATE_FIRST)
                      for g in range(ngrp) for jh in range(PEER_NKEYS // GATE_KEYS)]
        per_unit = len(gate_units) // len(mxu_units)
        for j, unit in enumerate(mxu_units):
            unit()
            for args in gate_units[j * per_unit:(j + 1) * per_unit]:
                gate_unit(*args)

    pl.when(k % 2 == 0)(functools.partial(roles, ht0_ref, ht1_ref, p1_ref, p0_ref))
    pl.when(k % 2 == 1)(functools.partial(roles, ht1_ref, ht0_ref, p0_ref, p1_ref))

    @pl.when((e_acc == nchunk - 1) & (k >= 2))
    def _():
        o_ref[...] = ho_ref[...] + acc_ref[...].T


def _peer_ffn(h1, g, r1, e1, c0, e0, u, vt, tb, ec):
    ntok = h1.shape[0]
    ngrp = tb // LANES
    nchunk = u.shape[0] // ec
    nsteps = (ntok // tb) * nchunk
    last = nsteps - 1
    at = lambda shift: (lambda k: jnp.clip(k - shift, 0, last))
    tok_blk = lambda shift: (lambda k: (at(shift)(k) // nchunk, 0))
    key_spec = pl.BlockSpec((PEER_HEADS, ngrp, PEER_NKEYS, LANES), lambda k: (0, at(1)(k) // nchunk, 0, 0))
    return pl.pallas_call(
        functools.partial(_peer_ffn_kernel, nchunk=nchunk, nsteps=nsteps),
        grid=(nsteps + 2,),
        in_specs=[pl.BlockSpec((tb, D_MODEL), tok_blk(0)),
                  pl.BlockSpec((tb, D_MODEL), tok_blk(2)),
                  pl.BlockSpec(g.shape, lambda k: (0, 0)),
                  key_spec, key_spec, key_spec, key_spec,
                  pl.BlockSpec((ec, D_MODEL), lambda k: (at(0)(k) % nchunk, 0)),
                  pl.BlockSpec((D_MODEL, ec), lambda k: (0, at(2)(k) % nchunk))],
        out_specs=pl.BlockSpec((tb, D_MODEL), tok_blk(2)),
        out_shape=jax.ShapeDtypeStruct((ntok, D_MODEL), F32),
        scratch_shapes=[pltpu.VMEM((D_MODEL, tb), BF16),
                        pltpu.VMEM((D_MODEL, tb), F32),
                        pltpu.VMEM((ngrp, ec, LANES), F32),
                        pltpu.VMEM((ngrp, ec, LANES), F32),
                        pltpu.VMEM((ngrp, ec, LANES), BF16),
                        pltpu.VMEM((ngrp, ec, LANES), BF16),
                        pltpu.VMEM((PEER_HEADS, ngrp, PEER_NKEYS, LANES), F32),
                        pltpu.VMEM((PEER_HEADS, ngrp, PEER_NKEYS, LANES), F32)],
        compiler_params=pltpu.CompilerParams(dimension_semantics=("arbitrary",), vmem_limit_bytes=VMEM_LIMIT),
        name="peer_ffn",
    )(h1, h1, g, r1, e1, c0, e0, u, vt)


def _ple_out_kernel(h_ref, p_ref, g_ref, wg_ref, wp_ref, fg_ref, o_ref, *, final):
    h = h_ref[...]
    gate = jax.nn.sigmoid(_dot(_rms(h, g_ref[...]).astype(BF16), wg_ref[...]))
    h = h + gate * _dot(p_ref[...].astype(BF16), wp_ref[...])
    o_ref[...] = _rms(h, fg_ref[...]) if final else h


def _ple_out(h, p, g, wg, wp, fg, tb, final):
    ntok = h.shape[0]
    full = lambda arr: pl.BlockSpec(arr.shape, lambda t: (0,) * arr.ndim)
    return pl.pallas_call(
        functools.partial(_ple_out_kernel, final=final),
        grid=(ntok // tb,),
        in_specs=[pl.BlockSpec((tb, D_MODEL), lambda t: (t, 0)), pl.BlockSpec((tb, PLE_DIM), lambda t: (t, 0)),
                  full(g), full(wg), full(wp), full(fg)],
        out_specs=pl.BlockSpec((tb, D_MODEL), lambda t: (t, 0)),
        out_shape=jax.ShapeDtypeStruct((ntok, D_MODEL), F32),
        compiler_params=pltpu.CompilerParams(dimension_semantics=("arbitrary",), vmem_limit_bytes=VMEM_LIMIT),
        name="ple_out",
    )(h, p, g, wg, wp, fg)


MIX_IN_TOKENS = 256
MIX_OUT_TOKENS = 512
TOPK_TOKENS = 1024
FFN_TOKENS = 512
FFN_EXPERTS = 1024
PLE_TOKENS = 512


def kernel(x, p, norm_mix_g, w_in, conv_w, conv_b, shift_mu, w0, w_up, a0, a_up, g_up, k_k, k_a, r_k, ln_x_g, ln_x_b,
           w_branch_a, w_branch_b, w_out, norm_ffn_g, peer_wq, peer_subkeys, peer_u, peer_v, norm_ple_g, ple_gate_w,
           ple_proj_w, final_norm_g):
    bsz, seq, _ = x.shape
    ntok = bsz * seq
    row = lambda t: t.reshape(1, -1)
    h = x
    for i in range(w_in.shape[0]):
        zeros_rank = jnp.zeros((DECAY_RANK, RWKV_DIM), F32)
        ma, sgb, r, lw, k, v, kk, a, g, bonus = _mix_in(
            h, row(norm_mix_g[i]), w_in[i].astype(BF16), conv_w[i], row(conv_b[i]), row(shift_mu[i]), row(w0[i]),
            jnp.concatenate([w_up[i], zeros_rank], axis=0), row(a0[i]),
            jnp.concatenate([zeros_rank, a_up[i]], axis=0), g_up[i], row(k_k[i]), row(k_a[i]), row(r_k[i]),
            w_branch_a[i].astype(BF16), MIX_IN_TOKENS)
        y = _rwkv_scan(r, lw, k, v, kk, a)
        flat = lambda t: t.reshape(ntok, t.shape[-1])
        h1 = _mix_out(flat(h), flat(y), flat(bonus), flat(g), flat(ma), flat(sgb), row(ln_x_g[i]), row(ln_x_b[i]),
                      w_branch_b[i].astype(BF16), w_out[i].astype(BF16), MIX_OUT_TOKENS)
        wq = peer_wq[i].astype(BF16).reshape(D_MODEL, PEER_HEADS, 2 * PEER_HALF).transpose(1, 0, 2)
        r1, e1, c0, e0 = _peer_topk(h1, row(norm_ffn_g[i]), wq,
                                    peer_subkeys[i].reshape(2 * PEER_HEADS, PEER_NKEYS, PEER_HALF), TOPK_TOKENS)
        h2 = _peer_ffn(h1, row(norm_ffn_g[i]), r1, e1, c0, e0, peer_u[i].astype(BF16),
                       peer_v[i].T.astype(BF16), FFN_TOKENS, FFN_EXPERTS)
        h = _ple_out(h2, flat(p[i]), row(norm_ple_g[i]), ple_gate_w[i].astype(BF16), ple_proj_w[i].astype(BF16),
                     row(final_norm_g), PLE_TOKENS, final=(i == w_in.shape[0] - 1))
        h = h.reshape(bsz, seq, D_MODEL)
    return h
```
